```python
import jax, jax.numpy as jnp
from jax import lax
import numpy as np

D_MODEL = 1024
BATCH = 32
SEQ = 2048
DEPTH = 1
DEC_BATCH = 8
DEC_SEQ = 2048
PAST_LEN = 128

MLA_HEADS = 8
QK_NOPE = 128
QK_ROPE = 64
QK_HEAD = QK_NOPE + QK_ROPE
V_HEAD = 128
Q_LORA = 256
KV_LORA = 256
ROPE_THETA = 10000.0
Q_BLOCK = 128
CONV_DIM = 512
CONV_WIDTH = 31
D_FF = 4 * D_MODEL
EPS = 1e-6
IN_COLS = Q_LORA + KV_LORA + QK_ROPE + 2 * CONV_DIM + 2 * D_MODEL
SPLITS = [Q_LORA, Q_LORA + KV_LORA, Q_LORA + KV_LORA + QK_ROPE, Q_LORA + KV_LORA + QK_ROPE + 2 * CONV_DIM]

kernel_name = "gated_mla_conformer_encoder"


def rms_norm(x, g):
    xf = x.astype(jnp.float32)
    y = xf * lax.rsqrt(jnp.mean(xf * xf, axis=-1, keepdims=True) + EPS)
    return (y * g.astype(jnp.float32)).astype(x.dtype)


def layer_norm(x, g, b):
    xf = x.astype(jnp.float32)
    mu = jnp.mean(xf, axis=-1, keepdims=True)
    xc = xf - mu
    var = jnp.mean(xc * xc, axis=-1, keepdims=True)
    y = xc * lax.rsqrt(var + EPS) * g.astype(jnp.float32) + b.astype(jnp.float32)
    return y.astype(x.dtype)


def rope_tables(seq_len):
    inv_freq = 1.0 / (ROPE_THETA ** (jnp.arange(0, QK_ROPE, 2, dtype=jnp.float32) / QK_ROPE))
    pos = jnp.arange(seq_len, dtype=jnp.float32)
    ang = pos[:, None] * inv_freq[None, :]
    ang = jnp.concatenate([ang, ang], axis=-1)
    return jnp.cos(ang), jnp.sin(ang)


def apply_rope(x, cos, sin):
    xf = x.astype(jnp.float32)
    x1, x2 = jnp.split(xf, 2, axis=-1)
    rot = jnp.concatenate([-x2, x1], axis=-1)
    return (xf * cos[None, :, None, :] + rot * sin[None, :, None, :]).astype(x.dtype)


def blocked_attention(q, k, v):
    b, s, h, dqk = q.shape
    n_blk = s // Q_BLOCK
    qb = q.reshape(b, n_blk, Q_BLOCK, h, dqk).transpose(1, 0, 2, 3, 4)
    scale = QK_HEAD ** -0.5

    def one_block(q_blk):
        sc = jnp.einsum('bqhd,bkhd->bhqk', q_blk, k, preferred_element_type=jnp.float32) * scale
        p = jax.nn.softmax(sc, axis=-1).astype(v.dtype)
        return jnp.einsum('bhqk,bkhv->bqhv', p, v)

    out = lax.map(one_block, qb)
    return out.transpose(1, 0, 2, 3, 4).reshape(b, s, h, v.shape[-1])


def encoder_layer(x, g_mix, w_in, g_q_lat, w_uq, g_kv_lat, w_ukv, g_qk_q, g_qk_k,
                  w_o_attn, w_dw, b_dw, g_conv_ln, b_conv_ln, w_o_conv, w_out,
                  g_ffn, w_ff1, w_ff2):
    b, s, _ = x.shape
    h = rms_norm(x, g_mix)
    proj = h @ w_in
    q_lat, kv_lat, k_pe, conv_in, gate_in = jnp.split(proj, SPLITS, axis=-1)

    cos, sin = rope_tables(s)
    q = (rms_norm(q_lat, g_q_lat) @ w_uq).reshape(b, s, MLA_HEADS, QK_HEAD)
    kv = (rms_norm(kv_lat, g_kv_lat) @ w_ukv).reshape(b, s, MLA_HEADS, QK_NOPE + V_HEAD)
    k_nope, v = jnp.split(kv, [QK_NOPE], axis=-1)
    k_pe_h = jnp.broadcast_to(k_pe[:, :, None, :], (b, s, MLA_HEADS, QK_ROPE))
    k = jnp.concatenate([k_nope, k_pe_h], axis=-1)
    q = rms_norm(q, g_qk_q)
    k = rms_norm(k, g_qk_k)
    q = jnp.concatenate([q[..., :QK_NOPE], apply_rope(q[..., QK_NOPE:], cos, sin)], axis=-1)
    k = jnp.concatenate([k[..., :QK_NOPE], apply_rope(k[..., QK_NOPE:], cos, sin)], axis=-1)
    attn = blocked_attention(q, k, v).reshape(b, s, MLA_HEADS * V_HEAD)
    attn_out = attn @ w_o_attn

    a, gt = jnp.split(conv_in, 2, axis=-1)
    u = a * jax.nn.sigmoid(gt)
    u = lax.conv_general_dilated(
        u, w_dw, window_strides=(1,),
        padding=[(CONV_WIDTH // 2, CONV_WIDTH // 2)],
        dimension_numbers=('NWC', 'WIO', 'NWC'),
        feature_group_count=CONV_DIM) + b_dw
    u = jax.nn.silu(layer_norm(u, g_conv_ln, b_conv_ln))
    conv_out = u @ w_o_conv

    gate_a, gate_c = jnp.split(jax.nn.sigmoid(gate_in), 2, axis=-1)
    x = x + (gate_a * attn_out + gate_c * conv_out) @ w_out

    hf = rms_norm(x, g_ffn)
    x = x + jnp.square(jax.nn.relu(hf @ w_ff1)) @ w_ff2
    return x


def setup_inputs(seed: int = 0) -> dict:
    key = jax.random.key(seed)
    ks = jax.random.split(key, 20)

    def w(k, shape, fan_in):
        return jax.random.normal(k, shape, jnp.float32) * (fan_in ** -0.5)

    def gain(k, shape):
        return 1.0 + 0.05 * jax.random.normal(k, shape, jnp.float32)

    def bias(k, shape):
        return 0.01 * jax.random.normal(k, shape, jnp.float32)

    L = DEPTH
    return {
        "x_prompt": jax.random.normal(ks[0], (BATCH, SEQ, D_MODEL), jnp.float32),
        "x_sample": jax.random.normal(ks[1], (DEC_BATCH, DEC_SEQ, D_MODEL), jnp.float32),
        "g_mix": gain(ks[2], (L, D_MODEL)),
        "w_in": w(ks[3], (L, D_MODEL, IN_COLS), D_MODEL),
        "g_q_lat": gain(ks[4], (L, Q_LORA)),
        "w_uq": w(ks[5], (L, Q_LORA, MLA_HEADS * QK_HEAD), Q_LORA),
        "g_kv_lat": gain(ks[6], (L, KV_LORA)),
        "w_ukv": w(ks[7], (L, KV_LORA, MLA_HEADS * (QK_NOPE + V_HEAD)), KV_LORA),
        "g_qk_q": gain(ks[8], (L, QK_HEAD)),
        "g_qk_k": gain(ks[9], (L, QK_HEAD)),
        "w_o_attn": w(ks[10], (L, MLA_HEADS * V_HEAD, D_MODEL), MLA_HEADS * V_HEAD),
        "w_dw": w(ks[11], (L, CONV_WIDTH, 1, CONV_DIM), CONV_WIDTH),
        "b_dw": bias(ks[12], (L, CONV_DIM)),
        "g_conv_ln": gain(ks[13], (L, CONV_DIM)),
        "b_conv_ln": bias(ks[14], (L, CONV_DIM)),
        "w_o_conv": w(ks[15], (L, CONV_DIM, D_MODEL), CONV_DIM),
        "w_out": w(ks[16], (L, D_MODEL, D_MODEL), D_MODEL),
        "g_ffn": gain(ks[17], (L, D_MODEL)),
        "w_ff1": w(ks[18], (L, D_MODEL, D_FF), D_MODEL),
        "w_ff2": w(ks[19], (L, D_FF, D_MODEL), D_FF),
    }


def reference(x_prompt, x_sample, g_mix, w_in, g_q_lat, w_uq, g_kv_lat, w_ukv, g_qk_q, g_qk_k,
              w_o_attn, w_dw, b_dw, g_conv_ln, b_conv_ln, w_o_conv, w_out, g_ffn, w_ff1, w_ff2):
    y_prompt = x_prompt
    y_sample = x_sample
    for l in range(DEPTH):
        params = (g_mix[l], w_in[l], g_q_lat[l], w_uq[l], g_kv_lat[l], w_ukv[l], g_qk_q[l], g_qk_k[l],
                  w_o_attn[l], w_dw[l], b_dw[l], g_conv_ln[l], b_conv_ln[l], w_o_conv[l], w_out[l],
                  g_ffn[l], w_ff1[l], w_ff2[l])
        y_prompt = encoder_layer(y_prompt, *params)
        y_sample = encoder_layer(y_sample, *params)
    return (y_prompt, y_sample)
```

```python
import functools
import math

import jax
import jax.numpy as jnp
from jax import lax
from jax.experimental import pallas as pl
from jax.experimental.pallas import tpu as pltpu

F32 = jnp.float32
BF16 = jnp.bfloat16

HEADS = 8
QK_NOPE = 128
QK_ROPE = 64
QK_HEAD = QK_NOPE + QK_ROPE
V_HEAD = 128
Q_LORA = 256
KV_LORA = 256
CONV_DIM = 512
CONV_WIDTH = 31
CONV_HALO = 16
ROPE_THETA = 10000.0
EPS = 1e-6
HEAD_COLS = 256
LANES = 128
VMEM_LIMIT = 56 * 1024 * 1024

Q_SCALE = (QK_HEAD ** -0.5) * math.log2(math.e)


def _sigmoid(x):
    return 0.5 * jnp.tanh(0.5 * x) + 0.5


def _rms(x, g):
    return x * lax.rsqrt(jnp.mean(x * x, axis=-1, keepdims=True) + EPS) * g


def _const_spec(shape):
    nd = len(shape)
    return pl.BlockSpec(shape, lambda *_: (0,) * nd, pipeline_mode=pl.Buffered(1))


def _input_kernel(x_ref, cs_ref, gmix_ref, win_ref, gql_ref, wuq_ref, gkvl_ref, wukv_ref,
                  gq_ref, gkn_ref, gkp_ref,
                  q_ref, k_ref, v_ref, u_ref, gate_ref, *, d_model):
    h = _rms(x_ref[...], gmix_ref[...]).astype(BF16)

    c_lat = Q_LORA + KV_LORA + 2 * QK_ROPE
    lat = jnp.dot(h, win_ref[:, 0:c_lat], preferred_element_type=F32)
    q_lat = lat[:, 0:Q_LORA]
    kv_lat = lat[:, Q_LORA:Q_LORA + KV_LORA]
    kpe = lat[:, Q_LORA + KV_LORA:c_lat]

    lane = lax.broadcasted_iota(jnp.int32, (1, LANES), 1)
    first_half = (lane < QK_ROPE).astype(F32)
    cs = cs_ref[...]
    inv_head = 1.0 / QK_HEAD

    qn = _rms(q_lat, gql_ref[...]).astype(BF16)
    gq = gq_ref[...] * Q_SCALE
    gq_nope = gq[:, 0:QK_NOPE]
    tq_rope = cs * gq[:, QK_NOPE:HEAD_COLS]
    for hd in range(HEADS):
        c0 = hd * HEAD_COLS
        qh = jnp.dot(qn, wuq_ref[:, c0:c0 + HEAD_COLS], preferred_element_type=F32)
        nope = qh[:, 0:QK_NOPE]
        rope = qh[:, QK_NOPE:HEAD_COLS]
        ss = jnp.sum(nope * nope + rope * rope * first_half, axis=-1, keepdims=True)
        r = lax.rsqrt(ss * inv_head + EPS)
        q_ref[:, c0:c0 + QK_NOPE] = (nope * r * gq_nope).astype(BF16)
        q_ref[:, c0 + QK_NOPE:c0 + HEAD_COLS] = (rope * r * tq_rope).astype(BF16)

    kvn = _rms(kv_lat, gkvl_ref[...]).astype(BF16)
    ss_pe = jnp.sum(kpe * kpe * first_half, axis=-1, keepdims=True)
    y = kpe * (cs * gkp_ref[...])
    k_rope = y + pltpu.roll(y, QK_ROPE, 1)
    gk_nope = gkn_ref[...]
    for hd in range(HEADS):
        c0 = hd * HEAD_COLS
        kvh = jnp.dot(kvn, wukv_ref[:, c0:c0 + HEAD_COLS], preferred_element_type=F32)
        k_nope = kvh[:, 0:QK_NOPE]
        ss = jnp.sum(k_nope * k_nope, axis=-1, keepdims=True) + ss_pe
        r = lax.rsqrt(ss * inv_head + EPS)
        k_ref[:, c0:c0 + QK_NOPE] = (k_nope * r * gk_nope).astype(BF16)
        k_ref[:, c0 + QK_NOPE:c0 + HEAD_COLS] = (k_rope * r).astype(BF16)
        v_ref[:, hd * V_HEAD:(hd + 1) * V_HEAD] = kvh[:, QK_NOPE:HEAD_COLS].astype(BF16)

    c_conv = c_lat
    ci = jnp.dot(h, win_ref[:, c_conv:c_conv + 2 * CONV_DIM], preferred_element_type=F32)
    u_ref[...] = (ci[:, 0:CONV_DIM] * _sigmoid(ci[:, CONV_DIM:2 * CONV_DIM])).astype(BF16)

    c_gate = c_conv + 2 * CONV_DIM
    chunk = 512
    for j in range(2 * d_model // chunk):
        g = jnp.dot(h, win_ref[:, c_gate + j * chunk:c_gate + (j + 1) * chunk],
                    preferred_element_type=F32)
        gate_ref[:, j * chunk:(j + 1) * chunk] = _sigmoid(g).astype(BF16)


def _input_stage(x2d, cs, p, *, seq, tm):
    t, d = x2d.shape
    n_seq_blk = seq // tm
    row = lambda w: pl.BlockSpec((tm, w), lambda i: (i, 0))
    out_shapes = (
        jax.ShapeDtypeStruct((t, HEADS * HEAD_COLS), BF16),
        jax.ShapeDtypeStruct((t, HEADS * HEAD_COLS), BF16),
        jax.ShapeDtypeStruct((t, HEADS * V_HEAD), BF16),
        jax.ShapeDtypeStruct((t, CONV_DIM), BF16),
        jax.ShapeDtypeStruct((t, 2 * d), BF16),
    )
    return pl.pallas_call(
        functools.partial(_input_kernel, d_model=d),
        grid=(t // tm,),
        in_specs=[
            row(d),
            pl.BlockSpec((tm, LANES), lambda i: (i % n_seq_blk, 0)),
            _const_spec((1, d)),
            _const_spec(p["w_in"].shape),
            _const_spec((1, Q_LORA)),
            _const_spec(p["w_uq"].shape),
            _const_spec((1, KV_LORA)),
            _const_spec(p["w_ukv"].shape),
            _const_spec((1, HEAD_COLS)),
            _const_spec((1, QK_NOPE)),
            _const_spec((1, LANES)),
        ],
        out_specs=[row(HEADS * HEAD_COLS), row(HEADS * HEAD_COLS), row(HEADS * V_HEAD),
                   row(CONV_DIM), row(2 * d)],
        out_shape=out_shapes,
        compiler_params=pltpu.CompilerParams(
            dimension_semantics=("arbitrary",), vmem_limit_bytes=VMEM_LIMIT),
        name="input_stage",
    )(x2d, cs, p["g_mix"], p["w_in"], p["g_q_lat"], p["w_uq"], p["g_kv_lat"], p["w_ukv"],
      p["g_q"], p["g_k_nope"], p["g_k_rope"])


def _attn_kernel(q_ref, k_ref, v_ref, o_ref, vext_ref):
    @pl.when(pl.program_id(2) == 0)
    def _():
        vext_ref[:, 0:V_HEAD] = v_ref[...]
        vext_ref[:, V_HEAD:2 * V_HEAD] = jnp.ones(v_ref.shape, BF16)

    s = lax.dot_general(q_ref[...], k_ref[...], (((1,), (1,)), ((), ())),
                        preferred_element_type=F32)
    m = jnp.max(s, axis=-1, keepdims=True)
    p = jnp.exp2(s - m).astype(BF16)
    o = jnp.dot(p, vext_ref[...], preferred_element_type=F32)
    o_ref[...] = (o[:, 0:V_HEAD] / o[:, V_HEAD:2 * V_HEAD]).astype(BF16)


def _attention(q, k, v, *, batch, seq, tq):
    t = batch * seq
    nq = seq // tq
    return pl.pallas_call(
        _attn_kernel,
        grid=(batch, HEADS, nq),
        in_specs=[
            pl.BlockSpec((tq, HEAD_COLS), lambda b, h, i: (b * nq + i, h)),
            pl.BlockSpec((seq, HEAD_COLS), lambda b, h, i: (b, h)),
            pl.BlockSpec((seq, V_HEAD), lambda b, h, i: (b, h)),
        ],
        out_specs=pl.BlockSpec((tq, V_HEAD), lambda b, h, i: (b * nq + i, h)),
        out_shape=jax.ShapeDtypeStruct((t, HEADS * V_HEAD), BF16),
        scratch_shapes=[pltpu.VMEM((seq, 2 * V_HEAD), BF16)],
        compiler_params=pltpu.CompilerParams(
            dimension_semantics=("arbitrary", "arbitrary", "arbitrary"),
            vmem_limit_bytes=VMEM_LIMIT),
        name="attention",
    )(q, k, v)


def _conv_kernel(u_ref, wdw_ref, bdw_ref, gln_ref, bln_ref, o_ref, win_ref, *, seq, tc, rows):
    j = pl.program_id(1)
    n_blk = seq // tc
    t0 = pl.multiple_of(j * tc, tc)
    prev_start = pl.multiple_of(jnp.maximum(t0 - CONV_HALO, 0), CONV_HALO)
    next_start = pl.multiple_of(jnp.minimum(t0 + tc, seq - CONV_HALO), CONV_HALO)
    has_prev = (j > 0).astype(F32)
    has_next = (j < n_blk - 1).astype(F32)
    win_ref[0:CONV_HALO, :] = u_ref[pl.ds(prev_start, CONV_HALO), :].astype(F32) * has_prev
    win_ref[CONV_HALO:CONV_HALO + tc, :] = u_ref[pl.ds(t0, tc), :].astype(F32)
    win_ref[CONV_HALO + tc:2 * CONV_HALO + tc, :] = (
        u_ref[pl.ds(next_start, CONV_HALO), :].astype(F32) * has_next)

    shift = CONV_HALO - CONV_WIDTH // 2
    bias = bdw_ref[...]
    gln = gln_ref[...]
    bln = bln_ref[...]
    for r in range(tc // rows):
        r0 = r * rows
        acc = jnp.broadcast_to(bias, (rows, CONV_DIM))
        for tap in range(CONV_WIDTH):
            acc = acc + win_ref[r0 + shift + tap:r0 + shift + tap + rows, :] * wdw_ref[tap:tap + 1, :]
        mu = jnp.mean(acc, axis=-1, keepdims=True)
        xc = acc - mu
        var = jnp.mean(xc * xc, axis=-1, keepdims=True)
        yv = xc * lax.rsqrt(var + EPS) * gln + bln
        o_ref[r0:r0 + rows, :] = (yv * _sigmoid(yv)).astype(BF16)


def _conv_stage(u, p, *, batch, seq, tc):
    t = batch * seq
    n_blk = seq // tc
    return pl.pallas_call(
        functools.partial(_conv_kernel, seq=seq, tc=tc, rows=32),
        grid=(batch, n_blk),
        in_specs=[
            pl.BlockSpec((seq, CONV_DIM), lambda b, j: (b, 0)),
            _const_spec((CONV_WIDTH, CONV_DIM)),
            _const_spec((1, CONV_DIM)),
            _const_spec((1, CONV_DIM)),
            _const_spec((1, CONV_DIM)),
        ],
        out_specs=pl.BlockSpec((tc, CONV_DIM), lambda b, j: (b * n_blk + j, 0)),
        out_shape=jax.ShapeDtypeStruct((t, CONV_DIM), BF16),
        scratch_shapes=[pltpu.VMEM((tc + 2 * CONV_HALO, CONV_DIM), F32)],
        compiler_params=pltpu.CompilerParams(
            dimension_semantics=("arbitrary", "arbitrary"), vmem_limit_bytes=VMEM_LIMIT),
        name="conv_stage",
    )(u, p["w_dw"], p["b_dw"], p["g_conv_ln"], p["b_conv_ln"])


def _output_kernel(x_ref, attn_ref, conv_ref, gate_ref, woa_ref, woc_ref, wout_ref, gffn_ref,
                   w1_ref, w2_ref, o_ref, *, d_model, d_ff, ff_chunk):
    attn_out = jnp.dot(attn_ref[...], woa_ref[...], preferred_element_type=F32)
    conv_out = jnp.dot(conv_ref[...], woc_ref[...], preferred_element_type=F32)
    merged = (gate_ref[:, 0:d_model].astype(F32) * attn_out
              + gate_ref[:, d_model:2 * d_model].astype(F32) * conv_out).astype(BF16)
    x1 = x_ref[...] + jnp.dot(merged, wout_ref[...], preferred_element_type=F32)
    hf = _rms(x1, gffn_ref[...]).astype(BF16)
    acc = x1
    for c in range(d_ff // ff_chunk):
        c0 = c * ff_chunk
        a = jnp.maximum(jnp.dot(hf, w1_ref[:, c0:c0 + ff_chunk], preferred_element_type=F32), 0.0)
        acc = acc + jnp.dot((a * a).astype(BF16), w2_ref[c0:c0 + ff_chunk, :],
                            preferred_element_type=F32)
    o_ref[...] = acc


def _output_stage(x2d, attn, conv, gates, p, *, tm):
    t, d = x2d.shape
    d_ff = p["w_ff1"].shape[1]
    row = lambda w: pl.BlockSpec((tm, w), lambda i: (i, 0))
    return pl.pallas_call(
        functools.partial(_output_kernel, d_model=d, d_ff=d_ff, ff_chunk=1024),
        grid=(t // tm,),
        in_specs=[
            row(d), row(HEADS * V_HEAD), row(CONV_DIM), row(2 * d),
            _const_spec(p["w_o_attn"].shape),
            _const_spec(p["w_o_conv"].shape),
            _const_spec(p["w_out"].shape),
            _const_spec((1, d)),
            _const_spec(p["w_ff1"].shape),
            _const_spec(p["w_ff2"].shape),
        ],
        out_specs=row(d),
        out_shape=jax.ShapeDtypeStruct((t, d), F32),
        compiler_params=pltpu.CompilerParams(
            dimension_semantics=("arbitrary",), vmem_limit_bytes=VMEM_LIMIT),
        name="output_stage",
    )(x2d, attn, conv, gates, p["w_o_attn"], p["w_o_conv"], p["w_out"], p["g_ffn"],
      p["w_ff1"], p["w_ff2"])


def _rotate_half_cols(w):
    half = w.shape[-1] // 2
    return jnp.concatenate([-w[..., half:], w[..., :half]], axis=-1)


def _swap_halves(g):
    half = g.shape[-1] // 2
    return jnp.concatenate([g[..., half:], g[..., :half]], axis=-1)


def _prepare_params(g_mix, w_in, g_q_lat, w_uq, g_kv_lat, w_ukv, g_qk_q, g_qk_k, w_o_attn, w_dw,
                    b_dw, g_conv_ln, b_conv_ln, w_o_conv, w_out, g_ffn, w_ff1, w_ff2):
    d = w_in.shape[0]
    c_pe = Q_LORA + KV_LORA
    w_kpe = w_in[:, c_pe:c_pe + QK_ROPE]
    w_in_ext = jnp.concatenate(
        [w_in[:, :c_pe + QK_ROPE], _rotate_half_cols(w_kpe), w_in[:, c_pe + QK_ROPE:]], axis=1)
    wq = w_uq.reshape(Q_LORA, HEADS, QK_HEAD)
    wq_rope = wq[..., QK_NOPE:]
    w_uq_ext = jnp.concatenate([wq[..., :QK_NOPE], wq_rope, _rotate_half_cols(wq_rope)], axis=-1)
    gq_rope = g_qk_q[QK_NOPE:]
    gk_rope = g_qk_k[QK_NOPE:]
    row = lambda g: g.reshape(1, -1).astype(F32)
    return {
        "g_mix": row(g_mix),
        "w_in": w_in_ext.astype(BF16),
        "g_q_lat": row(g_q_lat),
        "w_uq": w_uq_ext.reshape(Q_LORA, HEADS * HEAD_COLS).astype(BF16),
        "g_kv_lat": row(g_kv_lat),
        "w_ukv": w_ukv.astype(BF16),
        "g_q": row(jnp.concatenate([g_qk_q[:QK_NOPE], gq_rope, _swap_halves(gq_rope)])),
        "g_k_nope": row(g_qk_k[:QK_NOPE]),
        "g_k_rope": row(jnp.concatenate([gk_rope, _swap_halves(gk_rope)])),
        "w_o_attn": w_o_attn.astype(BF16),
        "w_dw": w_dw.reshape(CONV_WIDTH, CONV_DIM).astype(F32),
        "b_dw": row(b_dw),
        "g_conv_ln": row(g_conv_ln),
        "b_conv_ln": row(b_conv_ln),
        "w_o_conv": w_o_conv.astype(BF16),
        "w_out": w_out.astype(BF16),
        "g_ffn": row(g_ffn),
        "w_ff1": w_ff1.astype(BF16),
        "w_ff2": w_ff2.astype(BF16),
    }


def _rope_table(seq):
    inv_freq = 1.0 / (ROPE_THETA ** (jnp.arange(0, QK_ROPE, 2, dtype=F32) / QK_ROPE))
    ang = jnp.arange(seq, dtype=F32)[:, None] * inv_freq[None, :]
    ang = jnp.concatenate([ang, ang], axis=-1)
    return jnp.concatenate([jnp.cos(ang), jnp.sin(ang)], axis=-1)


def _encoder_layer(x, p, cs):
    b, s, d = x.shape
    x2d = x.reshape(b * s, d)
    q, k, v, u, gates = _input_stage(x2d, cs, p, seq=s, tm=512)
    attn = _attention(q, k, v, batch=b, seq=s, tq=512)
    conv = _conv_stage(u, p, batch=b, seq=s, tc=256)
    y = _output_stage(x2d, attn, conv, gates, p, tm=512)
    return y.reshape(b, s, d)


def kernel(x_prompt, x_sample, g_mix, w_in, g_q_lat, w_uq, g_kv_lat, w_ukv, g_qk_q, g_qk_k, w_o_attn, w_dw, b_dw, g_conv_ln, b_conv_ln, w_o_conv, w_out, g_ffn, w_ff1, w_ff2):
    stacked = (g_mix, w_in, g_q_lat, w_uq, g_kv_lat, w_ukv, g_qk_q, g_qk_k, w_o_attn, w_dw, b_dw,
               g_conv_ln, b_conv_ln, w_o_conv, w_out, g_ffn, w_ff1, w_ff2)
    y_prompt, y_sample = x_prompt, x_sample
    for layer in range(g_mix.shape[0]):
        p = _prepare_params(*(w[layer] for w in stacked))
        y_prompt = _encoder_layer(y_prompt, p, _rope_table(y_prompt.shape[1]))
        y_sample = _encoder_layer(y_sample, p, _rope_table(y_sample.shape[1]))
    return (y_prompt, y_sample)
```

```python
import functools
import math

import jax
import jax.numpy as jnp
from jax import lax
from jax.experimental import pallas as pl
from jax.experimental.pallas import tpu as pltpu

F32 = jnp.float32
BF16 = jnp.bfloat16

HEADS = 8
QK_NOPE = 128
QK_ROPE = 64
QK_HEAD = QK_NOPE + QK_ROPE
V_HEAD = 128
Q_LORA = 256
KV_LORA = 256
CONV_DIM = 512
CONV_WIDTH = 31
CONV_HALO = 16
ROPE_THETA = 10000.0
EPS = 1e-6
HEAD_COLS = 256
LANES = 128
SUBLANES = 8
KEY_TILE = 256
CONV_SUB = 128
CONV_ROWS = 32
VMEM_LIMIT = 56 * 1024 * 1024

Q_SCALE = (QK_HEAD ** -0.5) * math.log2(math.e)


def _sigmoid(x):
    return 0.5 * jnp.tanh(0.5 * x) + 0.5


def _rms(x, g):
    return x * lax.rsqrt(jnp.mean(x * x, axis=-1, keepdims=True) + EPS) * g


def _const_spec(shape):
    nd = len(shape)
    return pl.BlockSpec(shape, lambda *_: (0,) * nd, pipeline_mode=pl.Buffered(1))


def _input_kernel(x_ref, cs_ref, gmix_ref, win_ref, gql_ref, wuq_ref, gkvl_ref, wukv_ref,
                  gq_ref, gkn_ref, gkp_ref,
                  q_ref, k_ref, v_ref, u_ref, gate_ref, *, d_model):
    h = _rms(x_ref[...], gmix_ref[...]).astype(BF16)

    c_lat = Q_LORA + KV_LORA + 2 * QK_ROPE
    lat = jnp.dot(h, win_ref[:, 0:c_lat], preferred_element_type=F32)
    q_lat = lat[:, 0:Q_LORA]
    kv_lat = lat[:, Q_LORA:Q_LORA + KV_LORA]
    kpe = lat[:, Q_LORA + KV_LORA:c_lat]

    lane = lax.broadcasted_iota(jnp.int32, (1, LANES), 1)
    first_half = (lane < QK_ROPE).astype(F32)
    cs = cs_ref[...]
    inv_head = 1.0 / QK_HEAD

    qn = _rms(q_lat, gql_ref[...]).astype(BF16)
    gq = gq_ref[...] * Q_SCALE
    gq_nope = gq[:, 0:QK_NOPE]
    tq_rope = cs * gq[:, QK_NOPE:HEAD_COLS]
    for hd in range(HEADS):
        c0 = hd * HEAD_COLS
        qh = jnp.dot(qn, wuq_ref[:, c0:c0 + HEAD_COLS], preferred_element_type=F32)
        nope = qh[:, 0:QK_NOPE]
        rope = qh[:, QK_NOPE:HEAD_COLS]
        ss = jnp.sum(nope * nope + rope * rope * first_half, axis=-1, keepdims=True)
        r = lax.rsqrt(ss * inv_head + EPS)
        q_ref[:, c0:c0 + QK_NOPE] = (nope * r * gq_nope).astype(BF16)
        q_ref[:, c0 + QK_NOPE:c0 + HEAD_COLS] = (rope * r * tq_rope).astype(BF16)

    kvn = _rms(kv_lat, gkvl_ref[...]).astype(BF16)
    ss_pe = jnp.sum(kpe * kpe * first_half, axis=-1, keepdims=True)
    y = kpe * (cs * gkp_ref[...])
    k_rope = y + pltpu.roll(y, QK_ROPE, 1)
    gk_nope = gkn_ref[...]
    for hd in range(HEADS):
        c0 = hd * HEAD_COLS
        kvh = jnp.dot(kvn, wukv_ref[:, c0:c0 + HEAD_COLS], preferred_element_type=F32)
        k_nope = kvh[:, 0:QK_NOPE]
        ss = jnp.sum(k_nope * k_nope, axis=-1, keepdims=True) + ss_pe
        r = lax.rsqrt(ss * inv_head + EPS)
        k_ref[:, c0:c0 + QK_NOPE] = (k_nope * r * gk_nope).astype(BF16)
        k_ref[:, c0 + QK_NOPE:c0 + HEAD_COLS] = (k_rope * r).astype(BF16)
        v_ref[:, hd * V_HEAD:(hd + 1) * V_HEAD] = kvh[:, QK_NOPE:HEAD_COLS].astype(BF16)

    c_conv = c_lat
    ci = jnp.dot(h, win_ref[:, c_conv:c_conv + 2 * CONV_DIM], preferred_element_type=F32)
    u_ref[...] = (ci[:, 0:CONV_DIM] * _sigmoid(ci[:, CONV_DIM:2 * CONV_DIM])).astype(BF16)

    c_gate = c_conv + 2 * CONV_DIM
    chunk = 512
    for j in range(2 * d_model // chunk):
        g = jnp.dot(h, win_ref[:, c_gate + j * chunk:c_gate + (j + 1) * chunk],
                    preferred_element_type=F32)
        gate_ref[:, j * chunk:(j + 1) * chunk] = _sigmoid(g).astype(BF16)


def _input_stage(x2d, cs, p, *, seq, tm):
    t, d = x2d.shape
    n_seq_blk = seq // tm
    row = lambda w: pl.BlockSpec((tm, w), lambda i: (i, 0))
    out_shapes = (
        jax.ShapeDtypeStruct((t, HEADS * HEAD_COLS), BF16),
        jax.ShapeDtypeStruct((t, HEADS * HEAD_COLS), BF16),
        jax.ShapeDtypeStruct((t, HEADS * V_HEAD), BF16),
        jax.ShapeDtypeStruct((t, CONV_DIM), BF16),
        jax.ShapeDtypeStruct((t, 2 * d), BF16),
    )
    return pl.pallas_call(
        functools.partial(_input_kernel, d_model=d),
        grid=(t // tm,),
        in_specs=[
            row(d),
            pl.BlockSpec((tm, LANES), lambda i: (i % n_seq_blk, 0)),
            _const_spec((1, d)),
            _const_spec(p["w_in"].shape),
            _const_spec((1, Q_LORA)),
            _const_spec(p["w_uq"].shape),
            _const_spec((1, KV_LORA)),
            _const_spec(p["w_ukv"].shape),
            _const_spec((1, HEAD_COLS)),
            _const_spec((1, QK_NOPE)),
            _const_spec((1, LANES)),
        ],
        out_specs=[row(HEADS * HEAD_COLS), row(HEADS * HEAD_COLS), row(HEADS * V_HEAD),
                   row(CONV_DIM), row(2 * d)],
        out_shape=out_shapes,
        compiler_params=pltpu.CompilerParams(
            dimension_semantics=("arbitrary",), vmem_limit_bytes=VMEM_LIMIT),
        name="input_stage",
    )(x2d, cs, p["g_mix"], p["w_in"], p["g_q_lat"], p["w_uq"], p["g_kv_lat"], p["w_ukv"],
      p["g_q"], p["g_k_nope"], p["g_k_rope"])


def _attn_kernel(q_ref, k_ref, v_ref, o_ref, vext_ref, sa_ref, sb_ref, *, tq):
    n_pairs = q_ref.shape[0] // (2 * tq)
    vext_ref[:, 0:V_HEAD] = v_ref[...]
    vext_ref[:, V_HEAD:2 * V_HEAD] = jnp.ones(v_ref.shape, BF16)

    def scores(r, s_ref):
        rows = pl.ds(pl.multiple_of(r * tq, tq), tq)
        s_ref[...] = lax.dot_general(q_ref[rows, :], k_ref[...], (((1,), (1,)), ((), ())),
                                     preferred_element_type=F32)

    def softmax_pv(r, s_ref):
        rows = pl.ds(pl.multiple_of(r * tq, tq), tq)
        s = s_ref[...]
        m = jnp.max(s, axis=-1, keepdims=True)
        p = jnp.exp2(s - m).astype(BF16)
        o = jnp.dot(p, vext_ref[...], preferred_element_type=F32)
        o_ref[rows, :] = (o[:, 0:V_HEAD] / o[:, V_HEAD:2 * V_HEAD]).astype(BF16)

    scores(0, sa_ref)

    def body(j, carry):
        scores(2 * j + 1, sb_ref)
        softmax_pv(2 * j, sa_ref)
        scores(2 * j + 2, sa_ref)
        softmax_pv(2 * j + 1, sb_ref)
        return carry

    lax.fori_loop(0, n_pairs - 1, body, 0)
    last = 2 * n_pairs - 1
    scores(last, sb_ref)
    softmax_pv(last - 1, sa_ref)
    softmax_pv(last, sb_ref)


def _attention(q, k, v, *, batch, seq, tq):
    t = batch * seq
    return pl.pallas_call(
        functools.partial(_attn_kernel, tq=tq),
        grid=(batch, HEADS),
        in_specs=[
            pl.BlockSpec((seq, HEAD_COLS), lambda b, h: (b, h)),
            pl.BlockSpec((seq, HEAD_COLS), lambda b, h: (b, h)),
            pl.BlockSpec((seq, V_HEAD), lambda b, h: (b, h)),
        ],
        out_specs=pl.BlockSpec((seq, V_HEAD), lambda b, h: (b, h)),
        out_shape=jax.ShapeDtypeStruct((t, HEADS * V_HEAD), BF16),
        scratch_shapes=[pltpu.VMEM((seq, 2 * V_HEAD), BF16),
                        pltpu.VMEM((tq, seq), F32),
                        pltpu.VMEM((tq, seq), F32)],
        compiler_params=pltpu.CompilerParams(
            dimension_semantics=("arbitrary", "arbitrary"),
            vmem_limit_bytes=VMEM_LIMIT),
        name="attention",
    )(q, k, v)


def _conv_sub(sb, uwin_ref, wdw_ref, bdw_ref, gln_ref, bln_ref, shifted_ref, store):
    n_win = CONV_SUB + 2 * CONV_HALO
    first_tap_row = CONV_HALO - CONV_WIDTH // 2
    b0 = pl.multiple_of(sb * CONV_SUB, CONV_SUB)
    win = uwin_ref[pl.ds(b0, n_win), :].astype(F32)
    shifted_ref[0] = win
    for c in range(1, SUBLANES):
        shifted_ref[c] = pltpu.roll(win, n_win - c, 0)
    bias = bdw_ref[...]
    gln = gln_ref[...]
    bln = bln_ref[...]
    for r0 in range(0, CONV_SUB, CONV_ROWS):
        acc = jnp.broadcast_to(bias, (CONV_ROWS, CONV_DIM))
        for tap in range(CONV_WIDTH):
            a, c = divmod(first_tap_row + tap, SUBLANES)
            lo = r0 + a * SUBLANES
            x = shifted_ref[c, lo:lo + CONV_ROWS, :].reshape(CONV_ROWS // SUBLANES, SUBLANES, CONV_DIM)
            acc = acc + (x * wdw_ref[tap]).reshape(CONV_ROWS, CONV_DIM)
        mu = jnp.mean(acc, axis=-1, keepdims=True)
        xc = acc - mu
        var = jnp.mean(xc * xc, axis=-1, keepdims=True)
        yv = xc * lax.rsqrt(var + EPS) * gln + bln
        store(b0 + r0, (yv * _sigmoid(yv)).astype(BF16))


def _conv_kernel(ucur_ref, uprev_ref, unext_ref, wdw_ref, bdw_ref, gln_ref, bln_ref, o_ref,
                 uwin_ref, shifted_ref, *, seq):
    tc = ucur_ref.shape[0]
    t0 = pl.program_id(0) * tc
    has_prev = (t0 % seq != 0).astype(F32)
    has_next = ((t0 + tc) % seq != 0).astype(F32)
    uwin_ref[0:CONV_HALO, :] = (uprev_ref[...].astype(F32) * has_prev).astype(BF16)
    uwin_ref[CONV_HALO:CONV_HALO + tc, :] = ucur_ref[...]
    uwin_ref[CONV_HALO + tc:2 * CONV_HALO + tc, :] = (
        unext_ref[...].astype(F32) * has_next).astype(BF16)

    def store(row0, val):
        o_ref[pl.ds(row0, CONV_ROWS), :] = val

    for sb in range(tc // CONV_SUB):
        _conv_sub(sb, uwin_ref, wdw_ref, bdw_ref, gln_ref, bln_ref, shifted_ref, store)


def _conv_stage(u, p, *, seq, tc):
    t = u.shape[0]
    per = tc // CONV_HALO
    last = t // CONV_HALO - 1
    return pl.pallas_call(
        functools.partial(_conv_kernel, seq=seq),
        grid=(t // tc,),
        in_specs=[
            pl.BlockSpec((tc, CONV_DIM), lambda i: (i, 0)),
            pl.BlockSpec((CONV_HALO, CONV_DIM), lambda i: (jnp.maximum(i * per - 1, 0), 0)),
            pl.BlockSpec((CONV_HALO, CONV_DIM), lambda i: (jnp.minimum((i + 1) * per, last), 0)),
            _const_spec((CONV_WIDTH, SUBLANES, CONV_DIM)),
            _const_spec((1, CONV_DIM)),
            _const_spec((1, CONV_DIM)),
            _const_spec((1, CONV_DIM)),
        ],
        out_specs=pl.BlockSpec((tc, CONV_DIM), lambda i: (i, 0)),
        out_shape=jax.ShapeDtypeStruct((t, CONV_DIM), BF16),
        scratch_shapes=[pltpu.VMEM((tc + 2 * CONV_HALO, CONV_DIM), BF16),
                        pltpu.VMEM((SUBLANES, CONV_SUB + 2 * CONV_HALO, CONV_DIM), F32)],
        compiler_params=pltpu.CompilerParams(
            dimension_semantics=("arbitrary",), vmem_limit_bytes=VMEM_LIMIT),
        name="conv_stage",
    )(u, u, u, p["w_dw"], p["b_dw"], p["g_conv_ln"], p["b_conv_ln"])


def _output_kernel(x_ref, attn_ref, conv_ref, gate_ref, woa_ref, woc_ref, wout_ref, gffn_ref,
                   w1_ref, w2_ref, o_ref, *, d_model):
    attn_out = jnp.dot(attn_ref[...], woa_ref[...], preferred_element_type=F32)
    conv_out = jnp.dot(conv_ref[...], woc_ref[...], preferred_element_type=F32)
    merged = (gate_ref[:, 0:d_model].astype(F32) * attn_out
              + gate_ref[:, d_model:2 * d_model].astype(F32) * conv_out).astype(BF16)
    x1 = x_ref[...] + jnp.dot(merged, wout_ref[...], preferred_element_type=F32)
    hf = _rms(x1, gffn_ref[...]).astype(BF16)
    acc = x1
    for c in range(w1_ref.shape[0]):
        a = jnp.maximum(jnp.dot(hf, w1_ref[c], preferred_element_type=F32), 0.0)
        acc = acc + jnp.dot((a * a).astype(BF16), w2_ref[c], preferred_element_type=F32)
    o_ref[...] = acc


def _output_stage(x2d, attn, conv, gates, p, *, tm):
    t, d = x2d.shape
    row = lambda w: pl.BlockSpec((tm, w), lambda i: (i, 0))
    return pl.pallas_call(
        functools.partial(_output_kernel, d_model=d),
        grid=(t // tm,),
        in_specs=[
            row(d), row(HEADS * V_HEAD), row(CONV_DIM), row(2 * d),
            _const_spec(p["w_o_attn"].shape),
            _const_spec(p["w_o_conv"].shape),
            _const_spec(p["w_out"].shape),
            _const_spec((1, d)),
            _const_spec(p["w_ff1"].shape),
            _const_spec(p["w_ff2"].shape),
        ],
        out_specs=row(d),
        out_shape=jax.ShapeDtypeStruct((t, d), F32),
        compiler_params=pltpu.CompilerParams(
            dimension_semantics=("arbitrary",), vmem_limit_bytes=VMEM_LIMIT),
        name="output_stage",
    )(x2d, attn, conv, gates, p["w_o_attn"], p["w_o_conv"], p["w_out"], p["g_ffn"],
      p["w_ff1"], p["w_ff2"])


def _rotate_half_cols(w):
    half = w.shape[-1] // 2
    return jnp.concatenate([-w[..., half:], w[..., :half]], axis=-1)


def _swap_halves(g):
    half = g.shape[-1] // 2
    return jnp.concatenate([g[..., half:], g[..., :half]], axis=-1)


def _prepare_params(g_mix, w_in, g_q_lat, w_uq, g_kv_lat, w_ukv, g_qk_q, g_qk_k, w_o_attn, w_dw,
                    b_dw, g_conv_ln, b_conv_ln, w_o_conv, w_out, g_ffn, w_ff1, w_ff2, *, ff_chunk):
    d, d_ff = w_ff1.shape
    c_pe = Q_LORA + KV_LORA
    w_kpe = w_in[:, c_pe:c_pe + QK_ROPE]
    w_in_ext = jnp.concatenate(
        [w_in[:, :c_pe + QK_ROPE], _rotate_half_cols(w_kpe), w_in[:, c_pe + QK_ROPE:]], axis=1)
    wq = w_uq.reshape(Q_LORA, HEADS, QK_HEAD)
    wq_rope = wq[..., QK_NOPE:]
    w_uq_ext = jnp.concatenate([wq[..., :QK_NOPE], wq_rope, _rotate_half_cols(wq_rope)], axis=-1)
    gq_rope = g_qk_q[QK_NOPE:]
    gk_rope = g_qk_k[QK_NOPE:]
    row = lambda g: g.reshape(1, -1).astype(F32)
    return {
        "g_mix": row(g_mix),
        "w_in": w_in_ext.astype(BF16),
        "g_q_lat": row(g_q_lat),
        "w_uq": w_uq_ext.reshape(Q_LORA, HEADS * HEAD_COLS).astype(BF16),
        "g_kv_lat": row(g_kv_lat),
        "w_ukv": w_ukv.astype(BF16),
        "g_q": row(jnp.concatenate([g_qk_q[:QK_NOPE], gq_rope, _swap_halves(gq_rope)])),
        "g_k_nope": row(g_qk_k[:QK_NOPE]),
        "g_k_rope": row(jnp.concatenate([gk_rope, _swap_halves(gk_rope)])),
        "w_o_attn": w_o_attn.astype(BF16),
        "w_dw": jnp.broadcast_to(w_dw.reshape(CONV_WIDTH, 1, CONV_DIM).astype(F32),
                                 (CONV_WIDTH, SUBLANES, CONV_DIM)),
        "b_dw": row(b_dw),
        "g_conv_ln": row(g_conv_ln),
        "b_conv_ln": row(b_conv_ln),
        "w_o_conv": w_o_conv.astype(BF16),
        "w_out": w_out.astype(BF16),
        "g_ffn": row(g_ffn),
        "w_ff1": w_ff1.astype(BF16).reshape(d, d_ff // ff_chunk, ff_chunk).transpose(1, 0, 2),
        "w_ff2": w_ff2.astype(BF16).reshape(d_ff // ff_chunk, ff_chunk, d),
    }


def _rope_table(seq):
    inv_freq = 1.0 / (ROPE_THETA ** (jnp.arange(0, QK_ROPE, 2, dtype=F32) / QK_ROPE))
    ang = jnp.arange(seq, dtype=F32)[:, None] * inv_freq[None, :]
    ang = jnp.concatenate([ang, ang], axis=-1)
    return jnp.concatenate([jnp.cos(ang), jnp.sin(ang)], axis=-1)


def _encoder_layer(x, p, cs):
    b, s, d = x.shape
    x2d = x.reshape(b * s, d)
    q, k, v, u, gates = _input_stage(x2d, cs, p, seq=s, tm=512)
    attn = _attention(q, k, v, batch=b, seq=s, tq=256)
    conv = _conv_stage(u, p, seq=s, tc=512)
    y = _output_stage(x2d, attn, conv, gates, p, tm=512)
    return y.reshape(b, s, d)


def kernel(x_prompt, x_sample, g_mix, w_in, g_q_lat, w_uq, g_kv_lat, w_ukv, g_qk_q, g_qk_k, w_o_attn, w_dw, b_dw, g_conv_ln, b_conv_ln, w_o_conv, w_out, g_ffn, w_ff1, w_ff2):
    stacked = (g_mix, w_in, g_q_lat, w_uq, g_kv_lat, w_ukv, g_qk_q, g_qk_k, w_o_attn, w_dw, b_dw,
               g_conv_ln, b_conv_ln, w_o_conv, w_out, g_ffn, w_ff1, w_ff2)
    y_prompt, y_sample = x_prompt, x_sample
    for layer in range(g_mix.shape[0]):
        p = _prepare_params(*(w[layer] for w in stacked), ff_chunk=1024)
        y_prompt = _encoder_layer(y_prompt, p, _rope_table(y_prompt.shape[1]))
        y_sample = _encoder_layer(y_sample, p, _rope_table(y_sample.shape[1]))
    return (y_prompt, y_sample)
```

```python
import functools
import math

import jax
import jax.numpy as jnp
from jax import lax
from jax.experimental import pallas as pl
from jax.experimental.pallas import tpu as pltpu

F32 = jnp.float32
BF16 = jnp.bfloat16

HEADS = 8
QK_NOPE = 128
QK_ROPE = 64
QK_HEAD = QK_NOPE + QK_ROPE
V_HEAD = 128
Q_LORA = 256
KV_LORA = 256
CONV_DIM = 512
CONV_WIDTH = 31
CONV_HALO = 16
ROPE_THETA = 10000.0
EPS = 1e-6
HEAD_COLS = 256
LANES = 128
SUBLANES = 8
CONV_SUB = 128
CONV_ROWS = 32
FF_CHUNK = 1024
VMEM_LIMIT = 56 * 1024 * 1024

Q_SCALE = (QK_HEAD ** -0.5) * math.log2(math.e)


def _sigmoid(x):
    return 0.5 * jnp.tanh(0.5 * x) + 0.5


def _rms(x, g):
    return x * lax.rsqrt(jnp.mean(x * x, axis=-1, keepdims=True) + EPS) * g


def _const_spec(shape):
    nd = len(shape)
    return pl.BlockSpec(shape, lambda *_: (0,) * nd, pipeline_mode=pl.Buffered(1))


def _input_kernel(x_ref, cs_ref, gmix_ref, win_ref, gql_ref, wuq_ref, gkvl_ref, wukv_ref,
                  gq_ref, gkn_ref, gkp_ref,
                  q_ref, k_ref, v_ref, u_ref, gate_ref, *, d_model):
    h = _rms(x_ref[...], gmix_ref[...]).astype(BF16)

    c_lat = Q_LORA + KV_LORA + 2 * QK_ROPE
    lat = jnp.dot(h, win_ref[:, 0:c_lat], preferred_element_type=F32)
    q_lat = lat[:, 0:Q_LORA]
    kv_lat = lat[:, Q_LORA:Q_LORA + KV_LORA]
    kpe = lat[:, Q_LORA + KV_LORA:c_lat]

    lane = lax.broadcasted_iota(jnp.int32, (1, LANES), 1)
    first_half = (lane < QK_ROPE).astype(F32)
    cs = cs_ref[...]
    inv_head = 1.0 / QK_HEAD

    qn = _rms(q_lat, gql_ref[...]).astype(BF16)
    gq = gq_ref[...] * Q_SCALE
    gq_nope = gq[:, 0:QK_NOPE]
    tq_rope = cs * gq[:, QK_NOPE:HEAD_COLS]
    for hd in range(HEADS):
        c0 = hd * HEAD_COLS
        qh = jnp.dot(qn, wuq_ref[:, c0:c0 + HEAD_COLS], preferred_element_type=F32)
        nope = qh[:, 0:QK_NOPE]
        rope = qh[:, QK_NOPE:HEAD_COLS]
        ss = jnp.sum(nope * nope + rope * rope * first_half, axis=-1, keepdims=True)
        r = lax.rsqrt(ss * inv_head + EPS)
        q_ref[:, c0:c0 + QK_NOPE] = (nope * r * gq_nope).astype(BF16)
        q_ref[:, c0 + QK_NOPE:c0 + HEAD_COLS] = (rope * r * tq_rope).astype(BF16)

    kvn = _rms(kv_lat, gkvl_ref[...]).astype(BF16)
    ss_pe = jnp.sum(kpe * kpe * first_half, axis=-1, keepdims=True)
    y = kpe * (cs * gkp_ref[...])
    k_rope = y + pltpu.roll(y, QK_ROPE, 1)
    gk_nope = gkn_ref[...]
    for hd in range(HEADS):
        c0 = hd * HEAD_COLS
        kvh = jnp.dot(kvn, wukv_ref[:, c0:c0 + HEAD_COLS], preferred_element_type=F32)
        k_nope = kvh[:, 0:QK_NOPE]
        ss = jnp.sum(k_nope * k_nope, axis=-1, keepdims=True) + ss_pe
        r = lax.rsqrt(ss * inv_head + EPS)
        k_ref[:, c0:c0 + QK_NOPE] = (k_nope * r * gk_nope).astype(BF16)
        k_ref[:, c0 + QK_NOPE:c0 + HEAD_COLS] = (k_rope * r).astype(BF16)
        v_ref[:, hd * V_HEAD:(hd + 1) * V_HEAD] = kvh[:, QK_NOPE:HEAD_COLS].astype(BF16)

    c_conv = c_lat
    ci = jnp.dot(h, win_ref[:, c_conv:c_conv + 2 * CONV_DIM], preferred_element_type=F32)
    u_ref[...] = (ci[:, 0:CONV_DIM] * _sigmoid(ci[:, CONV_DIM:2 * CONV_DIM])).astype(BF16)

    c_gate = c_conv + 2 * CONV_DIM
    chunk = 512
    for j in range(2 * d_model // chunk):
        g = jnp.dot(h, win_ref[:, c_gate + j * chunk:c_gate + (j + 1) * chunk],
                    preferred_element_type=F32)
        gate_ref[:, j * chunk:(j + 1) * chunk] = _sigmoid(g).astype(BF16)


def _input_stage(x2d, cs, p, *, seq, tm):
    t, d = x2d.shape
    n_seq_blk = seq // tm
    row = lambda w: pl.BlockSpec((tm, w), lambda i: (i, 0))
    out_shapes = (
        jax.ShapeDtypeStruct((t, HEADS * HEAD_COLS), BF16),
        jax.ShapeDtypeStruct((t, HEADS * HEAD_COLS), BF16),
        jax.ShapeDtypeStruct((t, HEADS * V_HEAD), BF16),
        jax.ShapeDtypeStruct((t, CONV_DIM), BF16),
        jax.ShapeDtypeStruct((t, 2 * d), BF16),
    )
    return pl.pallas_call(
        functools.partial(_input_kernel, d_model=d),
        grid=(t // tm,),
        in_specs=[
            row(d),
            pl.BlockSpec((tm, LANES), lambda i: (i % n_seq_blk, 0)),
            _const_spec((1, d)),
            _const_spec(p["w_in"].shape),
            _const_spec((1, Q_LORA)),
            _const_spec(p["w_uq"].shape),
            _const_spec((1, KV_LORA)),
            _const_spec(p["w_ukv"].shape),
            _const_spec((1, HEAD_COLS)),
            _const_spec((1, QK_NOPE)),
            _const_spec((1, LANES)),
        ],
        out_specs=[row(HEADS * HEAD_COLS), row(HEADS * HEAD_COLS), row(HEADS * V_HEAD),
                   row(CONV_DIM), row(2 * d)],
        out_shape=out_shapes,
        compiler_params=pltpu.CompilerParams(
            dimension_semantics=("arbitrary",), vmem_limit_bytes=VMEM_LIMIT),
        name="input_stage",
    )(x2d, cs, p["g_mix"], p["w_in"], p["g_q_lat"], p["w_uq"], p["g_kv_lat"], p["w_ukv"],
      p["g_q"], p["g_k_nope"], p["g_k_rope"])


def _attn_kernel(q_ref, k_ref, v_ref, o_ref, vext_ref, sa_ref, sb_ref, *, tq, heads):
    seq = q_ref.shape[0]
    for hd in range(heads):
        vext_ref[hd, :, 0:V_HEAD] = v_ref[:, hd * V_HEAD:(hd + 1) * V_HEAD]
        vext_ref[hd, :, V_HEAD:2 * V_HEAD] = jnp.ones((seq, V_HEAD), BF16)

    def scores(item, s_ref):
        hd, r = item
        cols = slice(hd * HEAD_COLS, (hd + 1) * HEAD_COLS)
        s_ref[...] = lax.dot_general(q_ref[r * tq:(r + 1) * tq, cols], k_ref[:, cols],
                                     (((1,), (1,)), ((), ())),
                                     preferred_element_type=F32)

    def softmax_pv(item, s_ref):
        hd, r = item
        s = s_ref[...]
        m = jnp.max(s, axis=-1, keepdims=True)
        p = jnp.exp2(s - m).astype(BF16)
        o = jnp.dot(p, vext_ref[hd], preferred_element_type=F32)
        o_ref[r * tq:(r + 1) * tq, hd * V_HEAD:(hd + 1) * V_HEAD] = (
            o[:, 0:V_HEAD] / o[:, V_HEAD:2 * V_HEAD]).astype(BF16)

    items = [(hd, r) for hd in range(heads) for r in range(seq // tq)]
    bufs = (sa_ref, sb_ref)
    scores(items[0], bufs[0])
    for n, item in enumerate(items):
        if n + 1 < len(items):
            scores(items[n + 1], bufs[(n + 1) % 2])
        softmax_pv(item, bufs[n % 2])


def _attention(q, k, v, *, batch, seq, tq, heads):
    t = batch * seq
    return pl.pallas_call(
        functools.partial(_attn_kernel, tq=tq, heads=heads),
        grid=(batch, HEADS // heads),
        in_specs=[
            pl.BlockSpec((seq, heads * HEAD_COLS), lambda b, h: (b, h)),
            pl.BlockSpec((seq, heads * HEAD_COLS), lambda b, h: (b, h)),
            pl.BlockSpec((seq, heads * V_HEAD), lambda b, h: (b, h)),
        ],
        out_specs=pl.BlockSpec((seq, heads * V_HEAD), lambda b, h: (b, h)),
        out_shape=jax.ShapeDtypeStruct((t, HEADS * V_HEAD), BF16),
        scratch_shapes=[pltpu.VMEM((heads, seq, 2 * V_HEAD), BF16),
                        pltpu.VMEM((tq, seq), F32),
                        pltpu.VMEM((tq, seq), F32)],
        compiler_params=pltpu.CompilerParams(
            dimension_semantics=("arbitrary", "arbitrary"),
            vmem_limit_bytes=VMEM_LIMIT),
        name="attention",
    )(q, k, v)


def _conv_sub(sb, uwin_ref, wdw_ref, bdw_ref, gln_ref, bln_ref, shifted_ref, store):
    n_win = CONV_SUB + 2 * CONV_HALO
    first_tap_row = CONV_HALO - CONV_WIDTH // 2
    b0 = pl.multiple_of(sb * CONV_SUB, CONV_SUB)
    win = uwin_ref[pl.ds(b0, n_win), :].astype(F32)
    shifted_ref[0] = win
    for c in range(1, SUBLANES):
        shifted_ref[c] = pltpu.roll(win, n_win - c, 0)
    bias = bdw_ref[...]
    gln = gln_ref[...]
    bln = bln_ref[...]
    for r0 in range(0, CONV_SUB, CONV_ROWS):
        acc = jnp.broadcast_to(bias, (CONV_ROWS, CONV_DIM))
        for tap in range(CONV_WIDTH):
            a, c = divmod(first_tap_row + tap, SUBLANES)
            lo = r0 + a * SUBLANES
            x = shifted_ref[c, lo:lo + CONV_ROWS, :].reshape(CONV_ROWS // SUBLANES, SUBLANES, CONV_DIM)
            acc = acc + (x * wdw_ref[tap]).reshape(CONV_ROWS, CONV_DIM)
        mu = jnp.mean(acc, axis=-1, keepdims=True)
        xc = acc - mu
        var = jnp.mean(xc * xc, axis=-1, keepdims=True)
        yv = xc * lax.rsqrt(var + EPS) * gln + bln
        store(b0 + r0, (yv * _sigmoid(yv)).astype(BF16))


def _conv_kernel(ucur_ref, uprev_ref, unext_ref, wdw_ref, bdw_ref, gln_ref, bln_ref, o_ref,
                 uwin_ref, shifted_ref, *, seq):
    tc = ucur_ref.shape[0]
    t0 = pl.program_id(0) * tc
    has_prev = (t0 % seq != 0).astype(F32)
    has_next = ((t0 + tc) % seq != 0).astype(F32)
    uwin_ref[0:CONV_HALO, :] = (uprev_ref[...].astype(F32) * has_prev).astype(BF16)
    uwin_ref[CONV_HALO:CONV_HALO + tc, :] = ucur_ref[...]
    uwin_ref[CONV_HALO + tc:2 * CONV_HALO + tc, :] = (
        unext_ref[...].astype(F32) * has_next).astype(BF16)

    def store(row0, val):
        o_ref[pl.ds(row0, CONV_ROWS), :] = val

    for sb in range(tc // CONV_SUB):
        _conv_sub(sb, uwin_ref, wdw_ref, bdw_ref, gln_ref, bln_ref, shifted_ref, store)


def _conv_stage(u, p, *, seq, tc):
    t = u.shape[0]
    per = tc // CONV_HALO
    last = t // CONV_HALO - 1
    return pl.pallas_call(
        functools.partial(_conv_kernel, seq=seq),
        grid=(t // tc,),
        in_specs=[
            pl.BlockSpec((tc, CONV_DIM), lambda i: (i, 0)),
            pl.BlockSpec((CONV_HALO, CONV_DIM), lambda i: (jnp.maximum(i * per - 1, 0), 0)),
            pl.BlockSpec((CONV_HALO, CONV_DIM), lambda i: (jnp.minimum((i + 1) * per, last), 0)),
            _const_spec((CONV_WIDTH, SUBLANES, CONV_DIM)),
            _const_spec((1, CONV_DIM)),
            _const_spec((1, CONV_DIM)),
            _const_spec((1, CONV_DIM)),
        ],
        out_specs=pl.BlockSpec((tc, CONV_DIM), lambda i: (i, 0)),
        out_shape=jax.ShapeDtypeStruct((t, CONV_DIM), BF16),
        scratch_shapes=[pltpu.VMEM((tc + 2 * CONV_HALO, CONV_DIM), BF16),
                        pltpu.VMEM((SUBLANES, CONV_SUB + 2 * CONV_HALO, CONV_DIM), F32)],
        compiler_params=pltpu.CompilerParams(
            dimension_semantics=("arbitrary",), vmem_limit_bytes=VMEM_LIMIT),
        name="conv_stage",
    )(u, u, u, p["w_dw"], p["b_dw"], p["g_conv_ln"], p["b_conv_ln"])


def _output_kernel(x_ref, attn_ref, conv_ref, gate_ref, woa_ref, woc_ref, wout_ref, gffn_ref,
                   w1_ref, w2_ref, o_ref, *, d_model):
    attn_out = jnp.dot(attn_ref[...], woa_ref[...], preferred_element_type=F32)
    conv_out = jnp.dot(conv_ref[...], woc_ref[...], preferred_element_type=F32)
    merged = (gate_ref[:, 0:d_model].astype(F32) * attn_out
              + gate_ref[:, d_model:2 * d_model].astype(F32) * conv_out).astype(BF16)
    x1 = x_ref[...] + jnp.dot(merged, wout_ref[...], preferred_element_type=F32)
    hf = _rms(x1, gffn_ref[...]).astype(BF16)
    acc = x1
    for c0 in range(0, w1_ref.shape[1], FF_CHUNK):
        a = jnp.maximum(jnp.dot(hf, w1_ref[:, c0:c0 + FF_CHUNK], preferred_element_type=F32), 0.0)
        acc = acc + jnp.dot((a * a).astype(BF16), w2_ref[c0:c0 + FF_CHUNK, :],
                            preferred_element_type=F32)
    o_ref[...] = acc


def _output_stage(x2d, attn, conv, gates, p, *, tm):
    t, d = x2d.shape
    row = lambda w: pl.BlockSpec((tm, w), lambda i: (i, 0))
    return pl.pallas_call(
        functools.partial(_output_kernel, d_model=d),
        grid=(t // tm,),
        in_specs=[
            row(d), row(HEADS * V_HEAD), row(CONV_DIM), row(2 * d),
            _const_spec(p["w_o_attn"].shape),
            _const_spec(p["w_o_conv"].shape),
            _const_spec(p["w_out"].shape),
            _const_spec((1, d)),
            _const_spec(p["w_ff1"].shape),
            _const_spec(p["w_ff2"].shape),
        ],
        out_specs=row(d),
        out_shape=jax.ShapeDtypeStruct((t, d), F32),
        compiler_params=pltpu.CompilerParams(
            dimension_semantics=("arbitrary",), vmem_limit_bytes=VMEM_LIMIT),
        name="output_stage",
    )(x2d, attn, conv, gates, p["w_o_attn"], p["w_o_conv"], p["w_out"], p["g_ffn"],
      p["w_ff1"], p["w_ff2"])


def _rotate_half_cols(w):
    half = w.shape[-1] // 2
    return jnp.concatenate([-w[..., half:], w[..., :half]], axis=-1)


def _swap_halves(g):
    half = g.shape[-1] // 2
    return jnp.concatenate([g[..., half:], g[..., :half]], axis=-1)


def _prepare_params(g_mix, w_in, g_q_lat, w_uq, g_kv_lat, w_ukv, g_qk_q, g_qk_k, w_o_attn, w_dw,
                    b_dw, g_conv_ln, b_conv_ln, w_o_conv, w_out, g_ffn, w_ff1, w_ff2):
    c_pe = Q_LORA + KV_LORA
    w_kpe = w_in[:, c_pe:c_pe + QK_ROPE]
    w_in_ext = jnp.concatenate(
        [w_in[:, :c_pe + QK_ROPE], _rotate_half_cols(w_kpe), w_in[:, c_pe + QK_ROPE:]], axis=1)
    wq = w_uq.reshape(Q_LORA, HEADS, QK_HEAD)
    wq_rope = wq[..., QK_NOPE:]
    w_uq_ext = jnp.concatenate([wq[..., :QK_NOPE], wq_rope, _rotate_half_cols(wq_rope)], axis=-1)
    gq_rope = g_qk_q[QK_NOPE:]
    gk_rope = g_qk_k[QK_NOPE:]
    row = lambda g: g.reshape(1, -1).astype(F32)
    return {
        "g_mix": row(g_mix),
        "w_in": w_in_ext.astype(BF16),
        "g_q_lat": row(g_q_lat),
        "w_uq": w_uq_ext.reshape(Q_LORA, HEADS * HEAD_COLS).astype(BF16),
        "g_kv_lat": row(g_kv_lat),
        "w_ukv": w_ukv.astype(BF16),
        "g_q": row(jnp.concatenate([g_qk_q[:QK_NOPE], gq_rope, _swap_halves(gq_rope)])),
        "g_k_nope": row(g_qk_k[:QK_NOPE]),
        "g_k_rope": row(jnp.concatenate([gk_rope, _swap_halves(gk_rope)])),
        "w_o_attn": w_o_attn.astype(BF16),
        "w_dw": jnp.broadcast_to(w_dw.reshape(CONV_WIDTH, 1, CONV_DIM).astype(F32),
                                 (CONV_WIDTH, SUBLANES, CONV_DIM)),
        "b_dw": row(b_dw),
        "g_conv_ln": row(g_conv_ln),
        "b_conv_ln": row(b_conv_ln),
        "w_o_conv": w_o_conv.astype(BF16),
        "w_out": w_out.astype(BF16),
        "g_ffn": row(g_ffn),
        "w_ff1": w_ff1.astype(BF16),
        "w_ff2": w_ff2.astype(BF16),
    }


def _rope_table(seq):
    inv_freq = 1.0 / (ROPE_THETA ** (jnp.arange(0, QK_ROPE, 2, dtype=F32) / QK_ROPE))
    ang = jnp.arange(seq, dtype=F32)[:, None] * inv_freq[None, :]
    ang = jnp.concatenate([ang, ang], axis=-1)
    return jnp.concatenate([jnp.cos(ang), jnp.sin(ang)], axis=-1)


def _encoder_layer(x, p, cs):
    b, s, d = x.shape
    x2d = x.reshape(b * s, d)
    q, k, v, u, gates = _input_stage(x2d, cs, p, seq=s, tm=512)
    attn = _attention(q, k, v, batch=b, seq=s, tq=512, heads=2)
    conv = _conv_stage(u, p, seq=s, tc=512)
    y = _output_stage(x2d, attn, conv, gates, p, tm=512)
    return y.reshape(b, s, d)


def kernel(x_prompt, x_sample, g_mix, w_in, g_q_lat, w_uq, g_kv_lat, w_ukv, g_qk_q, g_qk_k, w_o_attn, w_dw, b_dw, g_conv_ln, b_conv_ln, w_o_conv, w_out, g_ffn, w_ff1, w_ff2):
    stacked = (g_mix, w_in, g_q_lat, w_uq, g_kv_lat, w_ukv, g_qk_q, g_qk_k, w_o_attn, w_dw, b_dw,
               g_conv_ln, b_conv_ln, w_o_conv, w_out, g_ffn, w_ff1, w_ff2)
    y_prompt, y_sample = x_prompt, x_sample
    for layer in range(g_mix.shape[0]):
        p = _prepare_params(*(w[layer] for w in stacked))
        y_prompt = _encoder_layer(y_prompt, p, _rope_table(y_prompt.shape[1]))
        y_sample = _encoder_layer(y_sample, p, _rope_table(y_sample.shape[1]))
    return (y_prompt, y_sample)
```
